```python
import jax, jax.numpy as jnp
from jax import lax
import numpy as np

D_MODEL = 1024
BATCH = 32
SEQ = 2048
DEPTH = 2
DEC_BATCH = 4
DEC_SEQ = 4096
PAST_LEN = 128

GRID_W = 64
ATTN_HEADS = 8
ATTN_HEAD_DIM = 64
ATTN_WIDTH = ATTN_HEADS * ATTN_HEAD_DIM
WIN_ROWS = 8
WIN_COLS = 16
QBLK_COLS = 16
KBLK_COLS = QBLK_COLS + WIN_COLS
N_CBLK = GRID_W // QBLK_COLS
SSM_HEADS = 8
SSM_HEAD_DIM = 64
SSM_WIDTH = SSM_HEADS * SSM_HEAD_DIM
SSM_GROUPS = 2
SSM_STATE = 128
CONV_W = 5
CONV_DIM = SSM_WIDTH + 2 * SSM_GROUPS * SSM_STATE
CHUNK = 128
D_MIX = ATTN_WIDTH + SSM_WIDTH
D_IN_PROJ = 3 * ATTN_WIDTH + SSM_WIDTH + CONV_DIM + 2 * SSM_HEADS
SPLITS = [ATTN_WIDTH, 2 * ATTN_WIDTH, 3 * ATTN_WIDTH, 3 * ATTN_WIDTH + SSM_WIDTH,
          3 * ATTN_WIDTH + SSM_WIDTH + CONV_DIM]
D_FF = 2816
ALPHA = (2 * DEPTH) ** 0.25
BETA = (8 * DEPTH) ** -0.25
LN_EPS = 1e-5
RMS_EPS = 1e-6

kernel_name = "hymba_natten_ssd_macaron_deepnorm_encoder"


def layer_norm(x, g, b):
    x32 = x.astype(jnp.float32)
    mu = jnp.mean(x32, axis=-1, keepdims=True)
    xc = x32 - mu
    var = jnp.mean(xc * xc, axis=-1, keepdims=True)
    y = xc * lax.rsqrt(var + LN_EPS) * g.astype(jnp.float32) + b.astype(jnp.float32)
    return y.astype(x.dtype)


def rms_norm(x, g):
    x32 = x.astype(jnp.float32)
    y = x32 * lax.rsqrt(jnp.mean(x32 * x32, axis=-1, keepdims=True) + RMS_EPS)
    return y * g.astype(jnp.float32)


def swiglu_ffn(x, w_gate, w_up, w_down):
    return (jax.nn.silu(x @ w_gate) * (x @ w_up)) @ w_down


def neighbourhood_attention(q, k, v, rpb):
    bsz, t, h, dh = q.shape
    rows = t // GRID_W
    kr = min(WIN_ROWS, rows)
    qcol = np.arange(GRID_W).reshape(N_CBLK, QBLK_COLS)
    kstart = np.clip(np.arange(N_CBLK) * QBLK_COLS - WIN_COLS // 2, 0, GRID_W - KBLK_COLS)
    kcol = kstart[:, None] + np.arange(KBLK_COLS)
    cstart = np.clip(qcol - WIN_COLS // 2, 0, GRID_W - WIN_COLS)
    col_valid = (kcol[:, None, :] >= cstart[:, :, None]) & (kcol[:, None, :] < cstart[:, :, None] + WIN_COLS)
    col_off = np.clip(kcol[:, None, :] - qcol[:, :, None] + WIN_COLS - 1, 0, 2 * WIN_COLS - 2)
    rpb_cols = jnp.where(col_valid, rpb[:, :, col_off].astype(jnp.float32), -jnp.inf)

    qg = q.reshape(bsz, rows, N_CBLK, QBLK_COLS, h, dh)
    kg_cols = k.reshape(bsz, rows, GRID_W, h, dh)[:, :, kcol]
    vg_cols = v.reshape(bsz, rows, GRID_W, h, dh)[:, :, kcol]

    def row_block(r):
        rs = jnp.clip(r - kr // 2, 0, rows - kr)
        k_blk = lax.dynamic_slice_in_dim(kg_cols, rs, kr, axis=1)
        v_blk = lax.dynamic_slice_in_dim(vg_cols, rs, kr, axis=1)
        q_blk = lax.dynamic_index_in_dim(qg, r, axis=1, keepdims=False)
        row_off = rs + jnp.arange(kr) - r + (WIN_ROWS - 1)
        bias = jnp.take(rpb_cols, row_off, axis=1).transpose(0, 2, 3, 1, 4)
        s = jnp.einsum('bjqhd,brjkhd->bhjqrk', q_blk, k_blk).astype(jnp.float32) + bias[None]
        p = jax.nn.softmax(s.reshape(s.shape[:4] + (kr * KBLK_COLS,)), axis=-1).reshape(s.shape)
        o = jnp.einsum('bhjqrk,brjkhd->bjqhd', p.astype(v.dtype), v_blk)
        return o.reshape(bsz, GRID_W, h, dh)

    out = lax.map(row_block, jnp.arange(rows))
    return jnp.moveaxis(out, 0, 1).reshape(bsz, t, h * dh)


def ssd_chunked(x, dt, a, b_mat, c_mat):
    bsz, t, h, p = x.shape
    g, n = b_mat.shape[2], b_mat.shape[3]
    e = h // g
    nc = t // CHUNK
    xd = (x.astype(jnp.float32) * dt[..., None]).reshape(bsz, nc, CHUNK, g, e, p)
    la = (dt * a).reshape(bsz, nc, CHUNK, g, e).transpose(0, 3, 4, 1, 2)
    a_cum = jnp.cumsum(la, axis=-1)
    bc = b_mat.astype(jnp.float32).reshape(bsz, nc, CHUNK, g, n)
    cc = c_mat.astype(jnp.float32).reshape(bsz, nc, CHUNK, g, n)
    cb = jnp.einsum('bclgn,bcsgn->bgcls', cc, bc)
    seg = a_cum[..., :, None] - a_cum[..., None, :]
    causal = np.tril(np.ones((CHUNK, CHUNK), dtype=bool))
    decay = jnp.exp(jnp.where(causal, seg, -jnp.inf))
    y_diag = jnp.einsum('bgcls,bgecls,bcsgep->bclgep', cb, decay, xd)
    decay_states = jnp.exp(a_cum[..., -1:] - a_cum)
    states = jnp.einsum('bclgn,bgecl,bclgep->bcgepn', bc, decay_states, xd)
    chunk_decay = jnp.exp(a_cum[..., -1])

    def step(hstate, inp):
        st, dec = inp
        return hstate * dec[..., None, None] + st, hstate

    _, prev = lax.scan(step, jnp.zeros_like(states[:, 0]),
                       (jnp.moveaxis(states, 1, 0), jnp.moveaxis(chunk_decay, 3, 0)))
    prev = jnp.moveaxis(prev, 0, 1)
    y_off = jnp.einsum('bclgn,bcgepn,bgecl->bclgep', cc, prev, jnp.exp(a_cum))
    return (y_diag + y_off).reshape(bsz, t, h, p)


def dwconv_centred(u, w, b):
    c = u.shape[-1]
    out = lax.conv_general_dilated(u, w[:, None, :].astype(u.dtype), window_strides=(1,),
                                   padding=[(CONV_W // 2, CONV_W // 2)],
                                   dimension_numbers=('NWC', 'WIO', 'NWC'), feature_group_count=c)
    return out + b


def ssd_mixer(z, xbc, dt_raw, conv_w, conv_b, dt_bias, a_log, d_skip, norm_g):
    bsz, t = z.shape[:2]
    xbc = jax.nn.silu(dwconv_centred(xbc, conv_w, conv_b))
    xs = xbc[..., :SSM_WIDTH].reshape(bsz, t, SSM_HEADS, SSM_HEAD_DIM)
    bm = xbc[..., SSM_WIDTH:SSM_WIDTH + SSM_GROUPS * SSM_STATE].reshape(bsz, t, SSM_GROUPS, SSM_STATE)
    cm = xbc[..., SSM_WIDTH + SSM_GROUPS * SSM_STATE:].reshape(bsz, t, SSM_GROUPS, SSM_STATE)
    dt = jax.nn.softplus(dt_raw.astype(jnp.float32).reshape(bsz, t, 2, SSM_HEADS) + dt_bias.astype(jnp.float32))
    a = -jnp.exp(a_log.astype(jnp.float32))
    y_f = ssd_chunked(xs, dt[:, :, 0], a[0], bm, cm)
    y_b = ssd_chunked(xs[:, ::-1], dt[:, ::-1, 1], a[1], bm[:, ::-1], cm[:, ::-1])[:, ::-1]
    y = y_f + y_b + d_skip.astype(jnp.float32)[:, None] * xs.astype(jnp.float32)
    y = y.reshape(bsz, t, SSM_WIDTH)
    return rms_norm(y * jax.nn.silu(z.astype(jnp.float32)), norm_g).astype(z.dtype)


def hybrid_mixer(x, w_in, conv_w, conv_b, dt_bias, a_log, d_skip, ssm_norm_g, attn_norm_g, rpb, w_out):
    bsz, t, _ = x.shape
    proj = x @ w_in
    q, k, v, z, xbc, dt_raw = jnp.split(proj, SPLITS, axis=-1)
    q = q.reshape(bsz, t, ATTN_HEADS, ATTN_HEAD_DIM) * (ATTN_HEAD_DIM ** -0.5)
    k = k.reshape(bsz, t, ATTN_HEADS, ATTN_HEAD_DIM)
    v = v.reshape(bsz, t, ATTN_HEADS, ATTN_HEAD_DIM)
    attn = rms_norm(neighbourhood_attention(q, k, v, rpb), attn_norm_g).astype(x.dtype)
    ssm = ssd_mixer(z, xbc, dt_raw, conv_w, conv_b, dt_bias, a_log, d_skip, ssm_norm_g)
    return jnp.concatenate([attn, ssm], axis=-1) @ w_out


def setup_inputs(seed: int = 0) -> dict:
    key = jax.random.key(seed)
    ks = jax.random.split(key, 26)
    nrm = lambda k, shape, s: jax.random.normal(k, shape, jnp.float32) * s
    dt0 = jnp.exp(jax.random.uniform(ks[10], (DEPTH, 2, SSM_HEADS), jnp.float32,
                                     minval=float(np.log(1e-3)), maxval=float(np.log(1e-1))))
    return {
        "x_prompt": nrm(ks[0], (BATCH, SEQ, D_MODEL), 1.0),
        "x_sample": nrm(ks[1], (DEC_BATCH, DEC_SEQ, D_MODEL), 1.0),
        "ffn1_w_gate": nrm(ks[2], (DEPTH, D_MODEL, D_FF), D_MODEL ** -0.5),
        "ffn1_w_up": nrm(ks[3], (DEPTH, D_MODEL, D_FF), D_MODEL ** -0.5),
        "ffn1_w_down": nrm(ks[4], (DEPTH, D_FF, D_MODEL), BETA * D_FF ** -0.5),
        "ln1_g": 1.0 + nrm(ks[5], (DEPTH, D_MODEL), 0.02),
        "ln1_b": nrm(ks[6], (DEPTH, D_MODEL), 0.02),
        "w_in": nrm(ks[7], (DEPTH, D_MODEL, D_IN_PROJ), D_MODEL ** -0.5),
        "conv_w": nrm(ks[8], (DEPTH, CONV_W, CONV_DIM), CONV_W ** -0.5),
        "conv_b": nrm(ks[9], (DEPTH, CONV_DIM), 0.01),
        "dt_bias": dt0 + jnp.log(-jnp.expm1(-dt0)),
        "a_log": jnp.log(jax.random.uniform(ks[11], (DEPTH, 2, SSM_HEADS), jnp.float32, minval=1.0, maxval=16.0)),
        "d_skip": 1.0 + nrm(ks[12], (DEPTH, SSM_HEADS), 0.02),
        "ssm_norm_g": 1.0 + nrm(ks[13], (DEPTH, SSM_WIDTH), 0.02),
        "attn_norm_g": 1.0 + nrm(ks[14], (DEPTH, ATTN_WIDTH), 0.02),
        "rpb": nrm(ks[15], (DEPTH, ATTN_HEADS, 2 * WIN_ROWS - 1, 2 * WIN_COLS - 1), 0.1),
        "w_out": nrm(ks[16], (DEPTH, D_MIX, D_MODEL), BETA * D_MIX ** -0.5),
        "ln2_g": 1.0 + nrm(ks[17], (DEPTH, D_MODEL), 0.02),
        "ln2_b": nrm(ks[18], (DEPTH, D_MODEL), 0.02),
        "ffn2_w_gate": nrm(ks[19], (DEPTH, D_MODEL, D_FF), D_MODEL ** -0.5),
        "ffn2_w_up": nrm(ks[20], (DEPTH, D_MODEL, D_FF), D_MODEL ** -0.5),
        "ffn2_w_down": nrm(ks[21], (DEPTH, D_FF, D_MODEL), BETA * D_FF ** -0.5),
        "ln3_g": 1.0 + nrm(ks[22], (DEPTH, D_MODEL), 0.02),
        "ln3_b": nrm(ks[23], (DEPTH, D_MODEL), 0.02),
    }


def reference(x_prompt, x_sample, ffn1_w_gate, ffn1_w_up, ffn1_w_down, ln1_g, ln1_b, w_in, conv_w, conv_b,
              dt_bias, a_log, d_skip, ssm_norm_g, attn_norm_g, rpb, w_out, ln2_g, ln2_b,
              ffn2_w_gate, ffn2_w_up, ffn2_w_down, ln3_g, ln3_b):
    def trunk(x):
        for l in range(DEPTH):
            x = layer_norm(ALPHA * x + 0.5 * swiglu_ffn(x, ffn1_w_gate[l], ffn1_w_up[l], ffn1_w_down[l]),
                           ln1_g[l], ln1_b[l])
            mix = hybrid_mixer(x, w_in[l], conv_w[l], conv_b[l], dt_bias[l], a_log[l], d_skip[l],
                               ssm_norm_g[l], attn_norm_g[l], rpb[l], w_out[l])
            x = layer_norm(ALPHA * x + mix, ln2_g[l], ln2_b[l])
            x = layer_norm(ALPHA * x + 0.5 * swiglu_ffn(x, ffn2_w_gate[l], ffn2_w_up[l], ffn2_w_down[l]),
                           ln3_g[l], ln3_b[l])
        return x

    y_prompt = trunk(x_prompt)
    y_sample = trunk(x_sample)
    return (y_prompt, y_sample)
```

```python
import functools

import numpy as np
import jax
import jax.numpy as jnp
from jax import lax
from jax.experimental import pallas as pl
from jax.experimental.pallas import tpu as pltpu

F32 = jnp.float32
BF16 = jnp.bfloat16

LANES = 128
SUBLANES = 8
MXU_N = 256

D_MODEL = 1024
DEPTH = 2
GRID_W = 64
ATTN_HEADS = 8
ATTN_HEAD_DIM = 64
ATTN_WIDTH = ATTN_HEADS * ATTN_HEAD_DIM
WIN_ROWS = 8
WIN_COLS = 16
SSM_HEADS = 8
SSM_HEAD_DIM = 64
SSM_WIDTH = SSM_HEADS * SSM_HEAD_DIM
SSM_GROUPS = 2
SSM_STATE = 128
CONV_W = 5
CONV_DIM = SSM_WIDTH + 2 * SSM_GROUPS * SSM_STATE
CHUNK = 128
D_FF = 2816
ALPHA = (2 * DEPTH) ** 0.25
LN_EPS = 1e-5
RMS_EPS = 1e-6

HEAD_PAIRS = ATTN_WIDTH // LANES
FF_CHUNK = MXU_N
N_FF_CHUNKS = D_FF // FF_CHUNK
HALO = SUBLANES
ROW_BLOCK = 8
N_BIAS_OFFS = 2 * WIN_ROWS - 2
TOKEN_TILE = 512

assert SSM_HEAD_DIM == ATTN_HEAD_DIM == LANES // 2
assert SSM_STATE == LANES and CHUNK == LANES
assert SSM_HEADS // SSM_GROUPS == 4


def _vmem_limit(block_bytes):
    return int(min(block_bytes + (16 << 20), 60 << 20))


def _const_spec(shape):
    zeros = (0,) * len(shape)
    return pl.BlockSpec(shape, lambda *_: zeros)


def _silu(x):
    return x / (1.0 + jnp.exp(-x))


def _layer_norm(y, g, b):
    mu = jnp.mean(y, axis=-1, keepdims=True)
    yc = y - mu
    var = jnp.mean(yc * yc, axis=-1, keepdims=True)
    return yc * lax.rsqrt(var + LN_EPS) * g + b


def _rms_norm(y, g):
    return y * lax.rsqrt(jnp.mean(y * y, axis=-1, keepdims=True) + RMS_EPS) * g


def _dot(a, b):
    return jnp.dot(a, b, preferred_element_type=F32)


def _ffn_ln_kernel(x_ref, wg_ref, wu_ref, wd_ref, g_ref, b_ref, o_ref, acc_ref):
    x = x_ref[...]
    xb = x.astype(BF16)
    acc_ref[...] = jnp.zeros_like(acc_ref)

    def body(j, carry):
        gate = _dot(xb, wg_ref[j])
        up = _dot(xb, wu_ref[j])
        h = (_silu(gate) * up).astype(BF16)
        acc_ref[...] += _dot(h, wd_ref[j])
        return carry

    lax.fori_loop(0, N_FF_CHUNKS, body, 0)
    o_ref[...] = _layer_norm(ALPHA * x + 0.5 * acc_ref[...], g_ref[...], b_ref[...])


def _ffn_ln(x, wg, wu, wd, g, b):
    n = x.shape[0]
    tm = TOKEN_TILE
    blocks = 2 * 2 * tm * D_MODEL * 4 + 2 * 3 * D_MODEL * D_FF * 2 + tm * D_MODEL * 4
    return pl.pallas_call(
        _ffn_ln_kernel,
        grid=(n // tm,),
        in_specs=[
            pl.BlockSpec((tm, D_MODEL), lambda i: (i, 0)),
            _const_spec(wg.shape), _const_spec(wu.shape), _const_spec(wd.shape),
            _const_spec(g.shape), _const_spec(b.shape),
        ],
        out_specs=pl.BlockSpec((tm, D_MODEL), lambda i: (i, 0)),
        out_shape=jax.ShapeDtypeStruct((n, D_MODEL), F32),
        scratch_shapes=[pltpu.VMEM((tm, D_MODEL), F32)],
        compiler_params=pltpu.CompilerParams(
            dimension_semantics=("arbitrary",), vmem_limit_bytes=_vmem_limit(blocks)),
        name="ffn_ln",
    )(x, wg, wu, wd, g, b)


def _in_proj_kernel(x_ref, wq_ref, wk_ref, wv_ref, wz_ref, wx_ref, wdt_ref,
                    q_ref, k_ref, v_ref, z_ref, xbc_ref, dt_ref):
    xb = x_ref[...].astype(BF16)
    q_ref[...] = (_dot(xb, wq_ref[...]) * (ATTN_HEAD_DIM ** -0.5)).astype(BF16)
    k_ref[...] = _dot(xb, wk_ref[...]).astype(BF16)
    v_ref[...] = _dot(xb, wv_ref[...]).astype(BF16)
    z_ref[...] = _dot(xb, wz_ref[...])
    xbc_ref[...] = _dot(xb, wx_ref[...])
    dt_ref[...] = _dot(xb, wdt_ref[...])


def _in_proj(x, wq, wk, wv, wz, wx, wdt):
    n = x.shape[0]
    tm = TOKEN_TILE
    widths = (ATTN_WIDTH, ATTN_WIDTH, ATTN_WIDTH, SSM_WIDTH, CONV_DIM, LANES)
    dtypes = (BF16, BF16, BF16, F32, F32, F32)
    w_bytes = sum(D_MODEL * w * 2 for w in widths)
    out_bytes = sum(tm * w * jnp.dtype(d).itemsize for w, d in zip(widths, dtypes))
    blocks = 2 * (tm * D_MODEL * 4 + w_bytes + out_bytes)
    tok = lambda w: pl.BlockSpec((tm, w), lambda i: (i, 0))
    return pl.pallas_call(
        _in_proj_kernel,
        grid=(n // tm,),
        in_specs=[tok(D_MODEL)] + [_const_spec(w.shape) for w in (wq, wk, wv, wz, wx, wdt)],
        out_specs=[tok(w) for w in widths],
        out_shape=[jax.ShapeDtypeStruct((n, w), d) for w, d in zip(widths, dtypes)],
        compiler_params=pltpu.CompilerParams(
            dimension_semantics=("arbitrary",), vmem_limit_bytes=_vmem_limit(blocks)),
        name="in_proj",
    )(x, wq, wk, wv, wz, wx, wdt)


def _attn_bias_table(rpb):
    qcol = np.arange(GRID_W)[:, None]
    kcol = np.arange(GRID_W)[None, :]
    cstart = np.clip(qcol - WIN_COLS // 2, 0, GRID_W - WIN_COLS)
    valid = (kcol >= cstart) & (kcol < cstart + WIN_COLS)
    col_off = np.clip(kcol - qcol + WIN_COLS - 1, 0, 2 * WIN_COLS - 2)
    bias = jnp.where(valid, rpb[:, :, col_off].astype(F32), -jnp.inf)
    two = jnp.concatenate([bias[:, :-1], bias[:, 1:]], axis=-1)
    two = two.reshape(HEAD_PAIRS, 2, N_BIAS_OFFS, GRID_W, LANES)
    return jnp.transpose(two, (0, 2, 1, 3, 4)).reshape(HEAD_PAIRS, N_BIAS_OFFS, 2 * GRID_W, LANES)


def _attn_kernel(q_ref, k_ref, v_ref, bias_ref, g_ref, o_ref, *, rows):
    blk = pl.program_id(1)
    lo = lax.broadcasted_iota(jnp.int32, (GRID_W, LANES), 1) < ATTN_HEAD_DIM
    win = WIN_ROWS * GRID_W

    def row_body(i, carry):
        r = blk * ROW_BLOCK + i
        rs = jnp.clip(r - WIN_ROWS // 2, 0, rows - WIN_ROWS)
        d = rs - r + (WIN_ROWS - 1)
        q_row = q_ref[0, pl.ds(pl.multiple_of(i * GRID_W, GRID_W), GRID_W), :]
        k0 = pl.multiple_of(rs * GRID_W, GRID_W)
        outs = []
        for p in range(HEAD_PAIRS):
            cols = slice(p * LANES, (p + 1) * LANES)
            qp = q_row[:, cols]
            zero = jnp.zeros_like(qp)
            q2 = jnp.concatenate([jnp.where(lo, qp, zero), jnp.where(lo, zero, qp)], axis=0)
            kp = k_ref[0, pl.ds(k0, win), cols]
            s = lax.dot_general(q2, kp, (((1,), (1,)), ((), ())), preferred_element_type=F32)
            s = s + jnp.concatenate([bias_ref[p, d + 2 * t] for t in range(WIN_ROWS // 2)], axis=1)
            m = jnp.max(s, axis=-1, keepdims=True)
            e = jnp.exp(s - m)
            l = jnp.sum(e, axis=-1, keepdims=True)
            vp = v_ref[0, pl.ds(k0, win), cols]
            o2 = _dot(e.astype(BF16), vp) / l
            outs.append(jnp.where(lo, o2[:GRID_W], o2[GRID_W:]))
        y = jnp.concatenate(outs, axis=1)
        o_ref[0, pl.ds(pl.multiple_of(i * GRID_W, GRID_W), GRID_W), :] = _rms_norm(y, g_ref[...])
        return carry

    lax.fori_loop(0, ROW_BLOCK, row_body, 0)


def _attention(q, k, v, bias, g):
    bsz, t, _ = q.shape
    rows = t // GRID_W
    assert rows >= WIN_ROWS and rows % ROW_BLOCK == 0
    tq = ROW_BLOCK * GRID_W
    blocks = 2 * (tq * ATTN_WIDTH * 2 + 2 * t * ATTN_WIDTH * 2 + bias.size * 4 + tq * ATTN_WIDTH * 4)
    return pl.pallas_call(
        functools.partial(_attn_kernel, rows=rows),
        grid=(bsz, rows // ROW_BLOCK),
        in_specs=[
            pl.BlockSpec((1, tq, ATTN_WIDTH), lambda b, j: (b, j, 0)),
            pl.BlockSpec((1, t, ATTN_WIDTH), lambda b, j: (b, 0, 0)),
            pl.BlockSpec((1, t, ATTN_WIDTH), lambda b, j: (b, 0, 0)),
            _const_spec(bias.shape), _const_spec(g.shape),
        ],
        out_specs=pl.BlockSpec((1, tq, ATTN_WIDTH), lambda b, j: (b, j, 0)),
        out_shape=jax.ShapeDtypeStruct((bsz, t, ATTN_WIDTH), F32),
        compiler_params=pltpu.CompilerParams(
            dimension_semantics=("arbitrary", "arbitrary"), vmem_limit_bytes=_vmem_limit(blocks)),
        name="nbr_attention",
    )(q, k, v, bias, g)


def _softplus(x):
    return jnp.maximum(x, 0.0) + jnp.log1p(jnp.exp(-jnp.abs(x)))


def _lane_scan(x, reverse):
    lane = lax.broadcasted_iota(jnp.int32, x.shape, 1)
    s = 1
    while s < LANES:
        if reverse:
            x = x + jnp.where(lane < LANES - s, pltpu.roll(x, LANES - s, 1), 0.0)
        else:
            x = x + jnp.where(lane >= s, pltpu.roll(x, s, 1), 0.0)
        s *= 2
    return x


def _pair_cols(mat, c0, lo):
    shape = (mat.shape[0], LANES)
    return jnp.where(lo, jnp.broadcast_to(mat[:, c0:c0 + 1], shape),
                     jnp.broadcast_to(mat[:, c0 + 1:c0 + 2], shape))


def _ssd_direction(x_ref, xp_ref, xn_ref, dt_ref, cw_ref, cb_ref, dtb_ref, alog_ref, dskip_ref,
                   y_ref, h_ref, stage_ref, chunk, nc, reverse):
    stage_ref[0:HALO, :] = jnp.where(chunk == 0, 0.0, xp_ref[0])
    stage_ref[HALO:HALO + CHUNK, :] = x_ref[0]
    stage_ref[HALO + CHUNK:, :] = jnp.where(chunk == nc - 1, 0.0, xn_ref[0])
    acc = jnp.broadcast_to(cb_ref[...], (CHUNK, CONV_DIM))
    base = HALO - CONV_W // 2
    for tap in range(CONV_W):
        acc = acc + stage_ref[base + tap:base + tap + CHUNK, :] * cw_ref[tap:tap + 1, :]
    xc = _silu(acc)

    dt = _softplus(dt_ref[0] + dtb_ref[...])
    la_t = (dt * (-jnp.exp(alog_ref[...]))).T
    col0 = SSM_HEADS if reverse else 0
    a_heads = _lane_scan(la_t[col0:col0 + SSM_HEADS], reverse)
    pad_lo = [jnp.zeros((col0, LANES), F32)] if col0 else []
    a_rows = jnp.concatenate(
        pad_lo + [a_heads, jnp.zeros((LANES - col0 - SSM_HEADS, LANES), F32)], axis=0)
    a_cols = a_rows.T
    edge = 0 if reverse else CHUNK - 1
    exp_a = jnp.exp(a_cols)
    dt_w = dt * jnp.exp(a_cols[edge:edge + 1, :] - a_cols)

    row = lax.broadcasted_iota(jnp.int32, (CHUNK, CHUNK), 0)
    col = lax.broadcasted_iota(jnp.int32, (CHUNK, CHUNK), 1)
    causal = (col >= row) if reverse else (col <= row)
    lo = col < SSM_HEAD_DIM
    t0 = pl.multiple_of(chunk * CHUNK, CHUNK)

    for g in range(SSM_GROUPS):
        b_mat = xc[:, SSM_WIDTH + g * SSM_STATE:SSM_WIDTH + (g + 1) * SSM_STATE]
        c_off = SSM_WIDTH + SSM_GROUPS * SSM_STATE + g * SSM_STATE
        c_mat = xc[:, c_off:c_off + SSM_STATE].astype(BF16)
        cb = lax.dot_general(c_mat, b_mat.astype(BF16), (((1,), (1,)), ((), ())),
                             preferred_element_type=F32)
        b_t = b_mat.T.astype(BF16)
        for pp in range(HEAD_PAIRS // SSM_GROUPS):
            p = g * (HEAD_PAIRS // SSM_GROUPS) + pp
            c0 = col0 + 2 * p
            cols = slice(p * LANES, (p + 1) * LANES)
            xs = xc[:, cols]
            xd = (xs * _pair_cols(dt, c0, lo)).astype(BF16)
            ms = []
            for hh in range(2):
                seg = a_cols[:, c0 + hh:c0 + hh + 1] - a_rows[c0 + hh:c0 + hh + 1, :]
                ms.append((cb * jnp.exp(jnp.where(causal, seg, -jnp.inf))).astype(BF16))
            yd = _dot(jnp.concatenate(ms, axis=0), xd)
            y = jnp.where(lo, yd[:CHUNK], yd[CHUNK:])
            scale_off = _pair_cols(exp_a, c0, lo)
            h_prev = h_ref[p]
            y = y + _dot(c_mat, h_prev.astype(BF16)) * scale_off
            if not reverse:
                y = y + xs * dskip_ref[:, cols]
            y_ref[0, pl.ds(t0, CHUNK), cols] += y
            xdw = (xs * _pair_cols(dt_w, c0, lo)).astype(BF16)
            h_ref[p] = h_prev * scale_off[edge:edge + 1, :] + _dot(b_t, xdw)


def _ssd_kernel(xf_ref, xfp_ref, xfn_ref, dtf_ref, xb_ref, xbp_ref, xbn_ref, dtb_ref,
                cw_ref, cb_ref, dtbias_ref, alog_ref, dskip_ref, y_ref,
                hf_ref, hb_ref, stage_ref, *, nc):
    c = pl.program_id(1)

    @pl.when(c == 0)
    def _():
        y_ref[...] = jnp.zeros_like(y_ref)
        hf_ref[...] = jnp.zeros_like(hf_ref)
        hb_ref[...] = jnp.zeros_like(hb_ref)

    params = (cw_ref, cb_ref, dtbias_ref, alog_ref, dskip_ref)
    _ssd_direction(xf_ref, xfp_ref, xfn_ref, dtf_ref, *params, y_ref, hf_ref, stage_ref, c, nc, False)
    _ssd_direction(xb_ref, xbp_ref, xbn_ref, dtb_ref, *params, y_ref, hb_ref, stage_ref,
                   nc - 1 - c, nc, True)


def _ssd(xbc, dt, cw, cb, dtbias, alog, dskip):
    bsz, t, _ = xbc.shape
    nc = t // CHUNK
    per = CHUNK // HALO
    n_halo = t // HALO
    fwd = lambda c: c
    bwd = lambda c: nc - 1 - c

    def specs(which):
        return [
            pl.BlockSpec((1, CHUNK, CONV_DIM), lambda b, c: (b, which(c), 0)),
            pl.BlockSpec((1, HALO, CONV_DIM), lambda b, c: (b, jnp.maximum(which(c) * per - 1, 0), 0)),
            pl.BlockSpec((1, HALO, CONV_DIM),
                         lambda b, c: (b, jnp.minimum((which(c) + 1) * per, n_halo - 1), 0)),
            pl.BlockSpec((1, CHUNK, LANES), lambda b, c: (b, which(c), 0)),
        ]

    consts = (cw, cb, dtbias, alog, dskip)
    blocks = (2 * 2 * ((CHUNK + 2 * HALO) * CONV_DIM + CHUNK * LANES) * 4 + 2 * t * SSM_WIDTH * 4
              + (2 * HEAD_PAIRS * SSM_STATE * LANES + (CHUNK + 2 * HALO) * CONV_DIM) * 4)
    return pl.pallas_call(
        functools.partial(_ssd_kernel, nc=nc),
        grid=(bsz, nc),
        in_specs=specs(fwd) + specs(bwd) + [_const_spec(a.shape) for a in consts],
        out_specs=pl.BlockSpec((1, t, SSM_WIDTH), lambda b, c: (b, 0, 0)),
        out_shape=jax.ShapeDtypeStruct((bsz, t, SSM_WIDTH), F32),
        scratch_shapes=[
            pltpu.VMEM((HEAD_PAIRS, SSM_STATE, LANES), F32),
            pltpu.VMEM((HEAD_PAIRS, SSM_STATE, LANES), F32),
            pltpu.VMEM((CHUNK + 2 * HALO, CONV_DIM), F32),
        ],
        compiler_params=pltpu.CompilerParams(
            dimension_semantics=("arbitrary", "arbitrary"), vmem_limit_bytes=_vmem_limit(blocks)),
        name="ssd_bidir",
    )(xbc, xbc, xbc, dt, xbc, xbc, xbc, dt, *consts)


def _out_proj_ln_kernel(x_ref, attn_ref, y_ref, z_ref, gs_ref, wa_ref, ws_ref, g_ref, b_ref, o_ref):
    ssm = _rms_norm(y_ref[...] * _silu(z_ref[...]), gs_ref[...])
    mix = _dot(attn_ref[...].astype(BF16), wa_ref[...]) + _dot(ssm.astype(BF16), ws_ref[...])
    o_ref[...] = _layer_norm(ALPHA * x_ref[...] + mix, g_ref[...], b_ref[...])


def _out_proj_ln(x, attn, y, z, gs, wa, ws, g, b):
    n = x.shape[0]
    tm = TOKEN_TILE
    blocks = 2 * (2 * tm * D_MODEL * 4 + 3 * tm * ATTN_WIDTH * 4 + 2 * ATTN_WIDTH * D_MODEL * 2)
    tok = lambda w: pl.BlockSpec((tm, w), lambda i: (i, 0))
    return pl.pallas_call(
        _out_proj_ln_kernel,
        grid=(n // tm,),
        in_specs=[tok(D_MODEL), tok(ATTN_WIDTH), tok(SSM_WIDTH), tok(SSM_WIDTH)]
        + [_const_spec(a.shape) for a in (gs, wa, ws, g, b)],
        out_specs=tok(D_MODEL),
        out_shape=jax.ShapeDtypeStruct((n, D_MODEL), F32),
        compiler_params=pltpu.CompilerParams(
            dimension_semantics=("arbitrary",), vmem_limit_bytes=_vmem_limit(blocks)),
        name="out_proj_ln",
    )(x, attn, y, z, gs, wa, ws, g, b)


def _ff_chunks_in(w):
    return jnp.transpose(w.astype(BF16).reshape(D_MODEL, N_FF_CHUNKS, FF_CHUNK), (1, 0, 2))


def _ff_chunks_out(w):
    return w.astype(BF16).reshape(N_FF_CHUNKS, FF_CHUNK, D_MODEL)


def _row(v):
    return v.astype(F32).reshape(1, -1)


def _pad_lanes(v):
    v = _row(v)
    return jnp.pad(v, ((0, 0), (0, LANES - v.shape[1])))


def _prepare_layer(l, p):
    w_in = p["w_in"][l].astype(BF16)
    a, s = ATTN_WIDTH, SSM_WIDTH
    w_dt = jnp.pad(w_in[:, 3 * a + s + CONV_DIM:], ((0, 0), (0, LANES - 2 * SSM_HEADS)))
    w_out = p["w_out"][l].astype(BF16)
    return dict(
        ffn1=(_ff_chunks_in(p["ffn1_w_gate"][l]), _ff_chunks_in(p["ffn1_w_up"][l]),
              _ff_chunks_out(p["ffn1_w_down"][l]), _row(p["ln1_g"][l]), _row(p["ln1_b"][l])),
        ffn2=(_ff_chunks_in(p["ffn2_w_gate"][l]), _ff_chunks_in(p["ffn2_w_up"][l]),
              _ff_chunks_out(p["ffn2_w_down"][l]), _row(p["ln3_g"][l]), _row(p["ln3_b"][l])),
        w_in=(w_in[:, :a], w_in[:, a:2 * a], w_in[:, 2 * a:3 * a], w_in[:, 3 * a:3 * a + s],
              w_in[:, 3 * a + s:3 * a + s + CONV_DIM], w_dt),
        bias=_attn_bias_table(p["rpb"][l]),
        attn_g=_row(p["attn_norm_g"][l]),
        ssd=(jnp.pad(p["conv_w"][l].astype(F32), ((0, SUBLANES - CONV_W), (0, 0))), _row(p["conv_b"][l]),
             _pad_lanes(p["dt_bias"][l].reshape(-1)), _pad_lanes(p["a_log"][l].reshape(-1)),
             _row(jnp.repeat(p["d_skip"][l], SSM_HEAD_DIM))),
        out=(_row(p["ssm_norm_g"][l]), w_out[:a], w_out[a:], _row(p["ln2_g"][l]), _row(p["ln2_b"][l])),
    )


def _trunk(x, layers):
    bsz, t, _ = x.shape
    h = x.reshape(bsz * t, D_MODEL)
    for lp in layers:
        h = _ffn_ln(h, *lp["ffn1"])
        q, k, v, z, xbc, dt = _in_proj(h, *lp["w_in"])
        seq = lambda a: a.reshape(bsz, t, a.shape[-1])
        attn = _attention(seq(q), seq(k), seq(v), lp["bias"], lp["attn_g"])
        y = _ssd(seq(xbc), seq(dt), *lp["ssd"])
        h = _out_proj_ln(h, attn.reshape(bsz * t, ATTN_WIDTH), y.reshape(bsz * t, SSM_WIDTH), z, *lp["out"])
        h = _ffn_ln(h, *lp["ffn2"])
    return h.reshape(bsz, t, D_MODEL)


def kernel(x_prompt, x_sample, ffn1_w_gate, ffn1_w_up, ffn1_w_down, ln1_g, ln1_b, w_in, conv_w, conv_b,
           dt_bias, a_log, d_skip, ssm_norm_g, attn_norm_g, rpb, w_out, ln2_g, ln2_b,
           ffn2_w_gate, ffn2_w_up, ffn2_w_down, ln3_g, ln3_b):
    params = dict(ffn1_w_gate=ffn1_w_gate, ffn1_w_up=ffn1_w_up, ffn1_w_down=ffn1_w_down, ln1_g=ln1_g,
                  ln1_b=ln1_b, w_in=w_in, conv_w=conv_w, conv_b=conv_b, dt_bias=dt_bias, a_log=a_log,
                  d_skip=d_skip, ssm_norm_g=ssm_norm_g, attn_norm_g=attn_norm_g, rpb=rpb, w_out=w_out,
                  ln2_g=ln2_g, ln2_b=ln2_b, ffn2_w_gate=ffn2_w_gate, ffn2_w_up=ffn2_w_up,
                  ffn2_w_down=ffn2_w_down, ln3_g=ln3_g, ln3_b=ln3_b)
    layers = [_prepare_layer(l, params) for l in range(DEPTH)]
    return (_trunk(x_prompt, layers), _trunk(x_sample, layers))
```

```python
import functools

import numpy as np
import jax
import jax.numpy as jnp
from jax import lax
from jax.experimental import pallas as pl
from jax.experimental.pallas import tpu as pltpu

F32 = jnp.float32
BF16 = jnp.bfloat16

LANES = 128
SUBLANES = 8
MXU_N = 256

D_MODEL = 1024
DEPTH = 2
GRID_W = 64
ATTN_HEADS = 8
ATTN_HEAD_DIM = 64
ATTN_WIDTH = ATTN_HEADS * ATTN_HEAD_DIM
WIN_ROWS = 8
WIN_COLS = 16
SSM_HEADS = 8
SSM_HEAD_DIM = 64
SSM_WIDTH = SSM_HEADS * SSM_HEAD_DIM
SSM_GROUPS = 2
SSM_STATE = 128
BC_WIDTH = SSM_GROUPS * SSM_STATE
CONV_W = 5
CONV_DIM = SSM_WIDTH + 2 * BC_WIDTH
CHUNK = 128
D_FF = 2816
ALPHA = (2 * DEPTH) ** 0.25
LN_EPS = 1e-5
RMS_EPS = 1e-6

HEAD_PAIRS = ATTN_WIDTH // LANES
PAIRS_PER_GROUP = HEAD_PAIRS // SSM_GROUPS
FF_CHUNK = MXU_N
N_FF_CHUNKS = D_FF // FF_CHUNK
FFN_SUB = 256
HALO = SUBLANES
ROW_BLOCK = 8
ATTN_GROUP = 2
N_BIAS_OFFS = 2 * WIN_ROWS - 2
TOKEN_TILE = 512
FFN_TILE = 1024
CONV_STRIDE = TOKEN_TILE // SUBLANES + 1
DT_ROWS = 2 * SSM_HEADS
CHUNKS_PER_TILE = TOKEN_TILE // CHUNK
LOG2E = float(np.log2(np.e))

assert SSM_HEAD_DIM == ATTN_HEAD_DIM == LANES // 2
assert SSM_STATE == LANES and CHUNK == LANES
assert SSM_HEADS == 2 * HEAD_PAIRS and SSM_HEADS == SUBLANES


def _vmem_limit(block_bytes):
    return int(min(block_bytes + (20 << 20), 60 << 20))


def _const_spec(shape):
    zeros = (0,) * len(shape)
    return pl.BlockSpec(shape, lambda *_: zeros, pipeline_mode=pl.Buffered(1))


def _silu(x):
    return x / (1.0 + jnp.exp(-x))


def _softplus(x):
    return jnp.maximum(x, 0.0) + jnp.log1p(jnp.exp(-jnp.abs(x)))


def _layer_norm(y, g, b):
    mu = jnp.mean(y, axis=-1, keepdims=True)
    yc = y - mu
    var = jnp.mean(yc * yc, axis=-1, keepdims=True)
    return yc * lax.rsqrt(var + LN_EPS) * g + b


def _rms_norm(y, g):
    return y * lax.rsqrt(jnp.mean(y * y, axis=-1, keepdims=True) + RMS_EPS) * g


def _dot(a, b):
    return jnp.dot(a, b, preferred_element_type=F32)


def _dot_nt(a, b):
    return lax.dot_general(a, b, (((1,), (1,)), ((), ())), preferred_element_type=F32)


def _ffn_ln_kernel(x_ref, wg_ref, wu_ref, wd_ref, g_ref, b_ref, o_ref):
    for sub in range(x_ref.shape[0] // FFN_SUB):
        rows = slice(sub * FFN_SUB, (sub + 1) * FFN_SUB)
        x = x_ref[rows, :]
        xb = x.astype(BF16)
        acc = None
        h_prev = None
        for j in range(N_FF_CHUNKS + 1):
            if j < N_FF_CHUNKS:
                cols = slice(j * FF_CHUNK, (j + 1) * FF_CHUNK)
                h_new = (_silu(_dot(xb, wg_ref[:, cols])) * _dot(xb, wu_ref[:, cols])).astype(BF16)
            if j > 0:
                down = _dot(h_prev, wd_ref[(j - 1) * FF_CHUNK:j * FF_CHUNK, :])
                acc = down if acc is None else acc + down
            h_prev = h_new
        o_ref[rows, :] = _layer_norm(ALPHA * x + 0.5 * acc, g_ref[...], b_ref[...])


def _ffn_ln(x, wg, wu, wd, g, b):
    n = x.shape[0]
    tm = FFN_TILE
    blocks = 2 * 2 * tm * D_MODEL * 4 + 3 * D_MODEL * D_FF * 2
    return pl.pallas_call(
        _ffn_ln_kernel,
        grid=(n // tm,),
        in_specs=[
            pl.BlockSpec((tm, D_MODEL), lambda i: (i, 0)),
            _const_spec(wg.shape), _const_spec(wu.shape), _const_spec(wd.shape),
            _const_spec(g.shape), _const_spec(b.shape),
        ],
        out_specs=pl.BlockSpec((tm, D_MODEL), lambda i: (i, 0)),
        out_shape=jax.ShapeDtypeStruct((n, D_MODEL), F32),
        compiler_params=pltpu.CompilerParams(
            dimension_semantics=("arbitrary",), vmem_limit_bytes=_vmem_limit(blocks)),
        name="ffn_ln",
    )(x, wg, wu, wd, g, b)


def _split3(x):
    hi = x.astype(BF16)
    r1 = x - hi.astype(F32)
    mid = r1.astype(BF16)
    return hi, mid, (r1 - mid.astype(F32)).astype(BF16)


def _in_proj_kernel(x_ref, xp_ref, xn_ref, wq_ref, wk_ref, wv_ref, wz_ref, wx_ref, wdt_ref, cw_ref, cb_ref,
                    dtb_ref, alog_ref, tri_ref,
                    q_ref, k_ref, v_ref, z_ref, xs_ref, b_ref, c_ref, acol_ref, rows_ref, u_scr, *, tiles_per_seq):
    tm = x_ref.shape[0]
    i = pl.program_id(0)
    first = (i % tiles_per_seq) == 0
    last = (i % tiles_per_seq) == tiles_per_seq - 1
    xb = jnp.concatenate([jnp.where(first, 0.0, xp_ref[...]), x_ref[...], jnp.where(last, 0.0, xn_ref[...])],
                         axis=0).astype(BF16)
    xt = x_ref[...].astype(BF16)
    dt_raw = _dot(xt, wdt_ref[...])
    u = _dot(xb, wx_ref[...])
    for lt in range(CONV_DIM // LANES):
        u_scr[lt] = u[:, lt * LANES:(lt + 1) * LANES]
    q_ref[...] = (_dot(xt, wq_ref[...]) * (ATTN_HEAD_DIM ** -0.5 * LOG2E)).astype(BF16)
    k_ref[...] = _dot(xt, wk_ref[...]).astype(BF16)
    v_ref[...] = _dot(xt, wv_ref[...]).astype(BF16)
    z_ref[...] = _dot(xt, wz_ref[...])

    dt = _softplus(dt_raw + dtb_ref[...])
    pieces = _split3(dt * (-jnp.exp(alog_ref[...]) * LOG2E))
    fwd_lane = lax.broadcasted_iota(jnp.int32, (CHUNK, LANES), 1) < SSM_HEADS
    for c in range(tm // CHUNK):
        rows = slice(c * CHUNK, (c + 1) * CHUNK)
        sums = _dot(tri_ref[...], jnp.concatenate([piece[rows] for piece in pieces], axis=1))
        sums = sums[:, :LANES] + sums[:, LANES:2 * LANES] + sums[:, 2 * LANES:]
        a_col = jnp.where(fwd_lane, sums[:CHUNK], sums[CHUNK:])
        acol_ref[rows, :] = a_col
        rows_ref[c, :DT_ROWS, :] = dt[rows].T[:DT_ROWS]
        rows_ref[c, DT_ROWS:, :] = a_col.T[:DT_ROWS]

    n_xs = SSM_WIDTH // LANES
    n_b = BC_WIDTH // LANES
    n_rows = tm + 2 * HALO
    for lt in range(CONV_DIM // LANES):
        lanes = slice(lt * LANES, (lt + 1) * LANES)
        taps = [cw_ref[tap:tap + 1, lanes] for tap in range(CONV_W)]
        ext = []
        for j in range(CONV_STRIDE + CONV_W - 1):
            start = HALO - CONV_W // 2 + j
            full = start + (SUBLANES - 1) * CONV_STRIDE < n_rows
            piece = u_scr[lt, pl.ds(start, SUBLANES if full else SUBLANES - 1, stride=CONV_STRIDE), :]
            ext.append(piece if full else jnp.concatenate([piece, jnp.zeros((1, LANES), F32)], axis=0))
        for j in range(CONV_STRIDE):
            acc = jnp.broadcast_to(cb_ref[:, lanes], (SUBLANES, LANES))
            for tap in range(CONV_W):
                acc = acc + ext[j + tap] * taps[tap]
            xc = _silu(acc)
            n_valid = SUBLANES if j + (SUBLANES - 1) * CONV_STRIDE < tm else SUBLANES - 1
            dst = pl.ds(j, n_valid, stride=CONV_STRIDE)
            if lt < n_xs:
                xs_ref[lt, dst, :] = xc[:n_valid]
            elif lt < n_xs + n_b:
                b_ref[lt - n_xs, dst, :] = xc[:n_valid]
            else:
                c_ref[lt - n_xs - n_b, dst, :] = xc[:n_valid]


def _scan_matrix():
    t = np.arange(CHUNK)[:, None]
    j = np.arange(CHUNK)[None, :]
    return jnp.asarray(np.concatenate([j <= t, j >= t], axis=0), BF16)


def _in_proj(x, seq_len, wq, wk, wv, wz, wx, wdt, cw, cb, dtb, alog):
    n = x.shape[0]
    tm = TOKEN_TILE
    assert seq_len % tm == 0
    per = tm // HALO
    n_halo = n // HALO
    tok_widths = (ATTN_WIDTH, ATTN_WIDTH, ATTN_WIDTH, SSM_WIDTH)
    tok_dtypes = (BF16, BF16, BF16, F32)
    tile_widths = (SSM_WIDTH, BC_WIDTH, BC_WIDTH)
    tri = _scan_matrix()
    w_bytes = sum(int(w.size) * 2 for w in (wq, wk, wv, wz, wx, wdt, tri))
    out_bytes = (sum(tm * w * jnp.dtype(d).itemsize for w, d in zip(tok_widths, tok_dtypes))
                 + tm * (sum(tile_widths) + LANES) * 4 + CHUNKS_PER_TILE * 2 * DT_ROWS * LANES * 4)
    blocks = 2 * ((tm + 2 * HALO) * D_MODEL * 4 + out_bytes) + w_bytes + (tm + 2 * HALO) * CONV_DIM * 4
    tok = lambda w: pl.BlockSpec((tm, w), lambda i: (i, 0))
    return pl.pallas_call(
        functools.partial(_in_proj_kernel, tiles_per_seq=seq_len // tm),
        grid=(n // tm,),
        in_specs=[
            tok(D_MODEL),
            pl.BlockSpec((HALO, D_MODEL), lambda i: (jnp.maximum(i * per - 1, 0), 0)),
            pl.BlockSpec((HALO, D_MODEL), lambda i: (jnp.minimum((i + 1) * per, n_halo - 1), 0)),
        ] + [_const_spec(w.shape) for w in (wq, wk, wv, wz, wx, wdt, cw, cb, dtb, alog, tri)],
        out_specs=[tok(w) for w in tok_widths]
        + [pl.BlockSpec((w // LANES, tm, LANES), lambda i: (0, i, 0)) for w in tile_widths]
        + [tok(LANES), pl.BlockSpec((CHUNKS_PER_TILE, 2 * DT_ROWS, LANES), lambda i: (i, 0, 0))],
        out_shape=[jax.ShapeDtypeStruct((n, w), d) for w, d in zip(tok_widths, tok_dtypes)]
        + [jax.ShapeDtypeStruct((w // LANES, n, LANES), F32) for w in tile_widths]
        + [jax.ShapeDtypeStruct((n, LANES), F32), jax.ShapeDtypeStruct((n // CHUNK, 2 * DT_ROWS, LANES), F32)],
        scratch_shapes=[pltpu.VMEM((CONV_DIM // LANES, tm + 2 * HALO, LANES), F32)],
        compiler_params=pltpu.CompilerParams(
            dimension_semantics=("arbitrary",), vmem_limit_bytes=_vmem_limit(blocks)),
        name="in_proj",
    )(x, x, x, wq, wk, wv, wz, wx, wdt, cw, cb, dtb, alog, tri)


def _attn_bias_table(rpb):
    qcol = np.arange(GRID_W)[:, None]
    kcol = np.arange(GRID_W)[None, :]
    cstart = np.clip(qcol - WIN_COLS // 2, 0, GRID_W - WIN_COLS)
    valid = (kcol >= cstart) & (kcol < cstart + WIN_COLS)
    col_off = np.clip(kcol - qcol + WIN_COLS - 1, 0, 2 * WIN_COLS - 2)
    bias = jnp.where(valid, rpb[:, :, col_off].astype(F32) * LOG2E, -jnp.inf)
    two = jnp.concatenate([bias[:, :-1], bias[:, 1:]], axis=-1)
    two = two.reshape(HEAD_PAIRS, 2, N_BIAS_OFFS, GRID_W, LANES)
    return jnp.transpose(two, (0, 2, 1, 3, 4)).reshape(HEAD_PAIRS, N_BIAS_OFFS, 2 * GRID_W, LANES)


def _attn_kernel(q_ref, k_ref, v_ref, bias_ref, g_ref, o_ref, *, rows):
    blk = pl.program_id(1)
    lo = lax.broadcasted_iota(jnp.int32, (GRID_W, LANES), 1) < ATTN_HEAD_DIM
    win = WIN_ROWS * GRID_W

    ones = jnp.ones((win, LANES), BF16)

    def scores(i):
        r = blk * ROW_BLOCK + i
        rs = jnp.clip(r - WIN_ROWS // 2, 0, rows - WIN_ROWS)
        d = rs - r + (WIN_ROWS - 1)
        k0 = pl.multiple_of(rs * GRID_W, GRID_W)
        q_row = q_ref[0, i * GRID_W:(i + 1) * GRID_W, :]
        es = []
        for p in range(HEAD_PAIRS):
            cols = slice(p * LANES, (p + 1) * LANES)
            qp = q_row[:, cols]
            zero = jnp.zeros_like(qp)
            q2 = jnp.concatenate([jnp.where(lo, qp, zero), jnp.where(lo, zero, qp)], axis=0)
            s = _dot_nt(q2, k_ref[0, pl.ds(k0, win), cols])
            s = s + jnp.concatenate([bias_ref[p, d + 2 * t] for t in range(WIN_ROWS // 2)], axis=1)
            es.append(jnp.exp2(s - jnp.max(s, axis=-1, keepdims=True)).astype(BF16))
        return i, k0, es

    def values(i, k0, es):
        outs = []
        for p in range(HEAD_PAIRS):
            cols = slice(p * LANES, (p + 1) * LANES)
            o2 = _dot(es[p], jnp.concatenate([v_ref[0, pl.ds(k0, win), cols], ones], axis=1))
            o2 = o2[:, :LANES] / o2[:, LANES:]
            outs.append(jnp.where(lo, o2[:GRID_W], o2[GRID_W:]))
        y = jnp.concatenate(outs, axis=1)
        o_ref[0, i * GRID_W:(i + 1) * GRID_W, :] = _rms_norm(y, g_ref[...])

    pending = []
    for i0 in range(0, ROW_BLOCK, ATTN_GROUP):
        current = [scores(i) for i in range(i0, i0 + ATTN_GROUP)]
        for args in pending:
            values(*args)
        pending = current
    for args in pending:
        values(*args)


def _attention(q, k, v, bias, g):
    bsz, t, _ = q.shape
    rows = t // GRID_W
    assert rows >= WIN_ROWS and rows % ROW_BLOCK == 0
    tq = ROW_BLOCK * GRID_W
    blocks = 2 * (tq * ATTN_WIDTH * 2 + 2 * t * ATTN_WIDTH * 2 + tq * ATTN_WIDTH * 4) + bias.size * 4
    return pl.pallas_call(
        functools.partial(_attn_kernel, rows=rows),
        grid=(bsz, rows // ROW_BLOCK),
        in_specs=[
            pl.BlockSpec((1, tq, ATTN_WIDTH), lambda b, j: (b, j, 0)),
            pl.BlockSpec((1, t, ATTN_WIDTH), lambda b, j: (b, 0, 0)),
            pl.BlockSpec((1, t, ATTN_WIDTH), lambda b, j: (b, 0, 0)),
            _const_spec(bias.shape), _const_spec(g.shape),
        ],
        out_specs=pl.BlockSpec((1, tq, ATTN_WIDTH), lambda b, j: (b, j, 0)),
        out_shape=jax.ShapeDtypeStruct((bsz, t, ATTN_WIDTH), F32),
        compiler_params=pltpu.CompilerParams(
            dimension_semantics=("arbitrary", "arbitrary"), vmem_limit_bytes=_vmem_limit(blocks)),
        name="nbr_attention",
    )(q, k, v, bias, g)


def _row_tile(mat, r):
    return jnp.broadcast_to(mat[r:r + 1, :], (CHUNK, LANES))


def _ssd_direction(xs_ref, b_ref, c_ref, acol_ref, rows_ref, dskip_ref, y_ref, h_ref, chunk, reverse):
    r0 = SSM_HEADS if reverse else 0
    edge = 0 if reverse else CHUNK - 1
    dt = rows_ref[0, r0:r0 + SSM_HEADS, :]
    a_row = rows_ref[0, DT_ROWS + r0:DT_ROWS + r0 + SSM_HEADS, :]
    a_col = acol_ref[...]
    a_edge = jnp.broadcast_to(a_row[:, edge:edge + 1], a_row.shape)
    w_dt = dt * jnp.exp2(a_edge - a_row)
    chunk_decay = jnp.exp2(a_edge)
    a_row_dt = a_row - jnp.log2(dt)

    row = lax.broadcasted_iota(jnp.int32, (CHUNK, CHUNK), 0)
    col = lax.broadcasted_iota(jnp.int32, (CHUNK, CHUNK), 1)
    causal = (col >= row) if reverse else (col <= row)
    lo = col < SSM_HEAD_DIM
    t0 = pl.multiple_of(chunk * CHUNK, CHUNK)

    for g in range(SSM_GROUPS):
        b_mat = b_ref[g]
        c_mat = c_ref[g].astype(BF16)
        cb = _dot_nt(c_mat, b_mat.astype(BF16))
        b_t = b_mat.T
        for pp in range(PAIRS_PER_GROUP):
            p = g * PAIRS_PER_GROUP + pp
            cols = slice(p * LANES, (p + 1) * LANES)
            xs = xs_ref[p]
            xsb = xs.astype(BF16)
            zero = jnp.zeros_like(xsb)
            xs2 = jnp.concatenate([jnp.where(lo, xsb, zero), jnp.where(lo, zero, xsb)], axis=0)
            ms, bts, exp_as, decays = [], [], [], []
            for h in (2 * p, 2 * p + 1):
                a_c = jnp.broadcast_to(a_col[:, r0 + h:r0 + h + 1], (CHUNK, LANES))
                seg = a_c - _row_tile(a_row_dt, h)
                ms.append((cb * jnp.exp2(jnp.where(causal, seg, -jnp.inf))).astype(BF16))
                bts.append((b_t * _row_tile(w_dt, h)).astype(BF16))
                exp_as.append(jnp.exp2(a_c))
                decays.append(_row_tile(chunk_decay, h))
            h_prev = h_ref[p]
            y = _dot(jnp.concatenate(ms, axis=1), xs2)
            y = y + _dot(c_mat, h_prev.astype(BF16)) * jnp.where(lo, exp_as[0], exp_as[1])
            if not reverse:
                y = y + xs * dskip_ref[:, cols]
            y_ref[0, pl.ds(t0, CHUNK), cols] += y
            h_ref[p] = h_prev * jnp.where(lo, decays[0], decays[1]) + _dot(jnp.concatenate(bts, axis=1), xs2)


def _ssd_kernel(xsf_ref, bf_ref, cf_ref, acolf_ref, rowsf_ref, xsb_ref, bb_ref, cb_ref, acolb_ref, rowsb_ref,
                dskip_ref, y_ref, hf_ref, hb_ref, *, nc):
    c = pl.program_id(1)

    @pl.when(c == 0)
    def _():
        y_ref[...] = jnp.zeros_like(y_ref)
        hf_ref[...] = jnp.zeros_like(hf_ref)
        hb_ref[...] = jnp.zeros_like(hb_ref)

    _ssd_direction(xsf_ref, bf_ref, cf_ref, acolf_ref, rowsf_ref, dskip_ref, y_ref, hf_ref, c, False)
    _ssd_direction(xsb_ref, bb_ref, cb_ref, acolb_ref, rowsb_ref, dskip_ref, y_ref, hb_ref, nc - 1 - c, True)


def _ssd(xs, bm, cm, acol, rows, dskip, bsz, t):
    nc = t // CHUNK

    def specs(which):
        chunk = lambda b, c: b * nc + which(c)
        tiles = lambda w: pl.BlockSpec((w // LANES, CHUNK, LANES), lambda b, c: (0, chunk(b, c), 0))
        return [tiles(SSM_WIDTH), tiles(BC_WIDTH), tiles(BC_WIDTH),
                pl.BlockSpec((CHUNK, LANES), lambda b, c: (chunk(b, c), 0)),
                pl.BlockSpec((1, 2 * DT_ROWS, LANES), lambda b, c: (chunk(b, c), 0, 0))]

    consts = (dskip,)
    step_bytes = CHUNK * (SSM_WIDTH + 2 * BC_WIDTH + LANES) * 4 + 2 * DT_ROWS * LANES * 4
    blocks = 2 * 2 * step_bytes + 2 * t * SSM_WIDTH * 4 + 2 * HEAD_PAIRS * SSM_STATE * LANES * 4
    return pl.pallas_call(
        functools.partial(_ssd_kernel, nc=nc),
        grid=(bsz, nc),
        in_specs=specs(lambda c: c) + specs(lambda c: nc - 1 - c) + [_const_spec(a.shape) for a in consts],
        out_specs=pl.BlockSpec((1, t, SSM_WIDTH), lambda b, c: (b, 0, 0)),
        out_shape=jax.ShapeDtypeStruct((bsz, t, SSM_WIDTH), F32),
        scratch_shapes=[
            pltpu.VMEM((HEAD_PAIRS, SSM_STATE, LANES), F32),
            pltpu.VMEM((HEAD_PAIRS, SSM_STATE, LANES), F32),
        ],
        compiler_params=pltpu.CompilerParams(
            dimension_semantics=("arbitrary", "arbitrary"), vmem_limit_bytes=_vmem_limit(blocks)),
        name="ssd_bidir",
    )(xs, bm, cm, acol, rows, xs, bm, cm, acol, rows, *consts)


def _out_proj_ln_kernel(x_ref, attn_ref, y_ref, z_ref, gs_ref, wa_ref, ws_ref, g_ref, b_ref, o_ref):
    ssm = _rms_norm(y_ref[...] * _silu(z_ref[...]), gs_ref[...])
    mix = _dot(attn_ref[...].astype(BF16), wa_ref[...]) + _dot(ssm.astype(BF16), ws_ref[...])
    o_ref[...] = _layer_norm(ALPHA * x_ref[...] + mix, g_ref[...], b_ref[...])


def _out_proj_ln(x, attn, y, z, gs, wa, ws, g, b):
    n = x.shape[0]
    tm = TOKEN_TILE
    blocks = 2 * (2 * tm * D_MODEL * 4 + 3 * tm * ATTN_WIDTH * 4) + 2 * ATTN_WIDTH * D_MODEL * 2
    tok = lambda w: pl.BlockSpec((tm, w), lambda i: (i, 0))
    return pl.pallas_call(
        _out_proj_ln_kernel,
        grid=(n // tm,),
        in_specs=[tok(D_MODEL), tok(ATTN_WIDTH), tok(SSM_WIDTH), tok(SSM_WIDTH)]
        + [_const_spec(a.shape) for a in (gs, wa, ws, g, b)],
        out_specs=tok(D_MODEL),
        out_shape=jax.ShapeDtypeStruct((n, D_MODEL), F32),
        compiler_params=pltpu.CompilerParams(
            dimension_semantics=("arbitrary",), vmem_limit_bytes=_vmem_limit(blocks)),
        name="out_proj_ln",
    )(x, attn, y, z, gs, wa, ws, g, b)


def _row(v):
    return v.astype(F32).reshape(1, -1)


def _head_lanes(v):
    return jnp.pad(v.astype(F32).reshape(1, DT_ROWS), ((0, 0), (0, LANES - DT_ROWS)))


def _prepare_layer(l, p):
    w_in = p["w_in"][l].astype(BF16)
    a, s = ATTN_WIDTH, SSM_WIDTH
    w_dt = jnp.pad(w_in[:, 3 * a + s + CONV_DIM:], ((0, 0), (0, LANES - DT_ROWS)))
    w_out = p["w_out"][l].astype(BF16)
    bf = lambda name: p[name][l].astype(BF16)
    return dict(
        ffn1=(bf("ffn1_w_gate"), bf("ffn1_w_up"), bf("ffn1_w_down"), _row(p["ln1_g"][l]), _row(p["ln1_b"][l])),
        ffn2=(bf("ffn2_w_gate"), bf("ffn2_w_up"), bf("ffn2_w_down"), _row(p["ln3_g"][l]), _row(p["ln3_b"][l])),
        w_in=(w_in[:, :a], w_in[:, a:2 * a], w_in[:, 2 * a:3 * a], w_in[:, 3 * a:3 * a + s],
              w_in[:, 3 * a + s:3 * a + s + CONV_DIM], w_dt,
              jnp.pad(p["conv_w"][l].astype(F32), ((0, SUBLANES - CONV_W), (0, 0))), _row(p["conv_b"][l]),
              _head_lanes(p["dt_bias"][l]), _head_lanes(p["a_log"][l])),
        bias=_attn_bias_table(p["rpb"][l]),
        attn_g=_row(p["attn_norm_g"][l]),
        dskip=_row(jnp.repeat(p["d_skip"][l], SSM_HEAD_DIM)),
        out=(_row(p["ssm_norm_g"][l]), w_out[:a], w_out[a:], _row(p["ln2_g"][l]), _row(p["ln2_b"][l])),
    )


def _trunk(x, layers):
    bsz, t, _ = x.shape
    h = x.reshape(bsz * t, D_MODEL)
    for lp in layers:
        h = _ffn_ln(h, *lp["ffn1"])
        q, k, v, z, xs, bm, cm, acol, rows = _in_proj(h, t, *lp["w_in"])
        seq = lambda a: a.reshape(bsz, t, a.shape[-1])
        attn = _attention(seq(q), seq(k), seq(v), lp["bias"], lp["attn_g"])
        y = _ssd(xs, bm, cm, acol, rows, lp["dskip"], bsz, t)
        h = _out_proj_ln(h, attn.reshape(bsz * t, ATTN_WIDTH), y.reshape(bsz * t, SSM_WIDTH), z, *lp["out"])
        h = _ffn_ln(h, *lp["ffn2"])
    return h.reshape(bsz, t, D_MODEL)


def kernel(x_prompt, x_sample, ffn1_w_gate, ffn1_w_up, ffn1_w_down, ln1_g, ln1_b, w_in, conv_w, conv_b,
           dt_bias, a_log, d_skip, ssm_norm_g, attn_norm_g, rpb, w_out, ln2_g, ln2_b,
           ffn2_w_gate, ffn2_w_up, ffn2_w_down, ln3_g, ln3_b):
    params = dict(ffn1_w_gate=ffn1_w_gate, ffn1_w_up=ffn1_w_up, ffn1_w_down=ffn1_w_down, ln1_g=ln1_g,
                  ln1_b=ln1_b, w_in=w_in, conv_w=conv_w, conv_b=conv_b, dt_bias=dt_bias, a_log=a_log,
                  d_skip=d_skip, ssm_norm_g=ssm_norm_g, attn_norm_g=attn_norm_g, rpb=rpb, w_out=w_out,
                  ln2_g=ln2_g, ln2_b=ln2_b, ffn2_w_gate=ffn2_w_gate, ffn2_w_up=ffn2_w_up,
                  ffn2_w_down=ffn2_w_down, ln3_g=ln3_g, ln3_b=ln3_b)
    layers = [_prepare_layer(l, params) for l in range(DEPTH)]
    return (_trunk(x_prompt, layers), _trunk(x_sample, layers))
```
